```python
import math
import jax, jax.numpy as jnp
from jax import lax
import numpy as np

D_MODEL = 2048
BATCH = 2
SEQ = 4096
DEPTH = 1
DEC_BATCH = 128
DEC_SEQ = 4
PAST_LEN = 16384
PAGE_SIZE = 128

SSM_WIDTH = D_MODEL // 2
GROUP_CH = 16
N_GROUPS = SSM_WIDTH // GROUP_CH
STATE_DIM = 64
HEAD_DIM = 64
N_HEADS = (D_MODEL // 2) // HEAD_DIM
N_KV_HEADS = 4
KV_GROUP = N_HEADS // N_KV_HEADS
ATTN_WIDTH = N_HEADS * HEAD_DIM
KV_WIDTH = N_KV_HEADS * HEAD_DIM
WINDOW = 128
NUM_BUCKETS = 32
MAX_DISTANCE = 128
FFN_HIDDEN = -(-8 * D_MODEL // (3 * 256)) * 256
IN_SPLITS = [SSM_WIDTH, ATTN_WIDTH, KV_WIDTH, KV_WIDTH, 2 * D_MODEL]
IN_COLS = sum(IN_SPLITS)
EPS = 1e-6
NEG_INF = -1e30

kernel_name = 'hybrid_s5_swa_sink_gated_decoder_step'


def rmsnorm(x, g):
    xf = x.astype(jnp.float32)
    y = xf * lax.rsqrt(jnp.mean(xf * xf, axis=-1, keepdims=True) + EPS)
    return (y * g.astype(jnp.float32)).astype(x.dtype)


def t5_bucket(dist):
    n = np.maximum(dist, 0)
    max_exact = NUM_BUCKETS // 2
    large = max_exact + (np.log(np.maximum(n, 1) / max_exact) / np.log(MAX_DISTANCE / max_exact)
                         * (NUM_BUCKETS - max_exact)).astype(np.int32)
    large = np.minimum(large, NUM_BUCKETS - 1)
    return np.where(n < max_exact, n, large).astype(np.int32)


def _complex_affine_combine(e1, e2):
    a1r, a1i, b1r, b1i = e1
    a2r, a2i, b2r, b2i = e2
    ar = a1r * a2r - a1i * a2i
    ai = a1r * a2i + a1i * a2r
    br = a2r * b1r - a2i * b1i + b2r
    bi = a2r * b1i + a2i * b1r + b2i
    return (ar, ai, br, bi)


def ssm_scan(u, h0_re, h0_im, lam_re, lam_im, log_dt, b_re, b_im, c_re, c_im, d_skip):
    f32 = jnp.float32
    n, L = u.shape[:2]
    lr = jnp.minimum(lam_re.astype(f32), -1e-4)
    li = lam_im.astype(f32)
    dt = jnp.exp(log_dt.astype(f32))[:, None]
    mag = jnp.exp(lr * dt)
    a_re = mag * jnp.cos(li * dt)
    a_im = mag * jnp.sin(li * dt)
    den = lr * lr + li * li
    coef_re = ((a_re - 1.0) * lr + a_im * li) / den
    coef_im = (a_im * lr - (a_re - 1.0) * li) / den
    uf = u.astype(f32)
    bu_re = jnp.einsum('nlgc,gpc->nlgp', uf, b_re.astype(f32))
    bu_im = jnp.einsum('nlgc,gpc->nlgp', uf, b_im.astype(f32))
    x_re = coef_re * bu_re - coef_im * bu_im
    x_im = coef_re * bu_im + coef_im * bu_re
    shp = x_re.shape
    elems = (jnp.broadcast_to(a_re, shp), jnp.broadcast_to(a_im, shp), x_re, x_im)
    acum_re, acum_im, h_re, h_im = lax.associative_scan(_complex_affine_combine, elems, axis=1)
    if h0_re is not None:
        s_re = h0_re.astype(f32)[:, None]
        s_im = h0_im.astype(f32)[:, None]
        h_re, h_im = (h_re + acum_re * s_re - acum_im * s_im,
                      h_im + acum_re * s_im + acum_im * s_re)
    y = (jnp.einsum('nlgp,gcp->nlgc', h_re, c_re.astype(f32))
         - jnp.einsum('nlgp,gcp->nlgc', h_im, c_im.astype(f32))
         + d_skip.astype(f32).reshape(N_GROUPS, GROUP_CH) * uf)
    return y.reshape(n, L, SSM_WIDTH), h_re[:, -1], h_im[:, -1]


def window_attention(q, k, v, dist, valid, rel_bias, sinks):
    lead = q.shape[:-3]
    nq = q.shape[-3]
    qg = q.reshape(*lead, nq, N_KV_HEADS, KV_GROUP, HEAD_DIM)
    s = jnp.einsum('...qhgd,...khd->...hgqk', qg, k,
                   preferred_element_type=jnp.float32) * (HEAD_DIM ** -0.5)
    bias = rel_bias.astype(jnp.float32)[t5_bucket(dist)]
    bias = jnp.transpose(bias, (2, 0, 1)).reshape(N_KV_HEADS, KV_GROUP, *dist.shape)
    s = jnp.where(valid, s + bias, NEG_INF)
    sink = jnp.broadcast_to(sinks.astype(jnp.float32).reshape(N_KV_HEADS, KV_GROUP, 1, 1),
                            s.shape[:-1] + (1,))
    p = jax.nn.softmax(jnp.concatenate([s, sink], axis=-1), axis=-1)[..., :-1]
    o = jnp.einsum('...hgqk,...khd->...qhgd', p.astype(v.dtype), v)
    return o.reshape(*lead, nq, ATTN_WIDTH)


def prompt_window_attention(q, k, v, rel_bias, sinks):
    b, L = q.shape[:2]
    nb = L // WINDOW
    qb = q.reshape(b, nb, WINDOW, N_HEADS, HEAD_DIM)
    kb = k.reshape(b, nb, WINDOW, N_KV_HEADS, HEAD_DIM)
    vb = v.reshape(b, nb, WINDOW, N_KV_HEADS, HEAD_DIM)
    pad_k = jnp.zeros_like(kb[:, :1])
    pad_v = jnp.zeros_like(vb[:, :1])
    kk = jnp.concatenate([jnp.concatenate([pad_k, kb[:, :-1]], axis=1), kb], axis=2)
    vv = jnp.concatenate([jnp.concatenate([pad_v, vb[:, :-1]], axis=1), vb], axis=2)
    i = np.arange(WINDOW)[:, None]
    j = np.arange(2 * WINDOW)[None, :]
    dist = i + WINDOW - j
    kpos = (np.arange(nb)[:, None, None] - 1) * WINDOW + j
    valid = (dist >= 0) & (dist < WINDOW) & (kpos >= 0)
    o = window_attention(qb, kk, vv, dist, valid[:, None, None], rel_bias, sinks)
    return o.reshape(b, L, ATTN_WIDTH)


def sample_window_attention(q, k, v, k_buf, v_buf, rel_bias, sinks):
    w_buf = k_buf.shape[1]
    nq = q.shape[1]
    kk = jnp.concatenate([k_buf.astype(k.dtype), k], axis=1)
    vv = jnp.concatenate([v_buf.astype(v.dtype), v], axis=1)
    dist = np.arange(nq)[:, None] + w_buf - np.arange(w_buf + nq)[None, :]
    valid = (dist >= 0) & (dist < WINDOW)
    o = window_attention(q, kk, vv, dist, valid, rel_bias, sinks)
    return o, kk[:, -w_buf:], vv[:, -w_buf:]


def decoder_layer(x, h0_re, h0_im, k_buf, v_buf, rel_bias,
                  norm_attn, w_in, lam_re, lam_im, log_dt, b_re, b_im, c_re, c_im, d_skip,
                  w_glu_val, w_glu_gate, w_attn_br, sinks, w_out, norm_ffn, w_ffn_in, w_ffn_out):
    n, L, _ = x.shape
    h = rmsnorm(x, norm_attn)
    proj = h @ w_in
    u, q, k, v, gates = jnp.split(proj, list(np.cumsum(IN_SPLITS)[:-1]), axis=-1)
    y, st_re, st_im = ssm_scan(u.reshape(n, L, N_GROUPS, GROUP_CH), h0_re, h0_im,
                               lam_re, lam_im, log_dt, b_re, b_im, c_re, c_im, d_skip)
    g = jax.nn.gelu(y).astype(x.dtype)
    a_out = (g @ w_glu_val) * jax.nn.sigmoid(g @ w_glu_gate)
    q = q.reshape(n, L, N_HEADS, HEAD_DIM)
    k = k.reshape(n, L, N_KV_HEADS, HEAD_DIM)
    v = v.reshape(n, L, N_KV_HEADS, HEAD_DIM)
    if k_buf is None:
        w_buf = min(WINDOW, PAST_LEN)
        o = prompt_window_attention(q, k, v, rel_bias, sinks)
        new_k, new_v = k[:, L - w_buf:], v[:, L - w_buf:]
    else:
        o, new_k, new_v = sample_window_attention(q, k, v, k_buf, v_buf, rel_bias, sinks)
    b_out = o @ w_attn_br
    g_a, g_b = jnp.split(jax.nn.sigmoid(gates), 2, axis=-1)
    x = x + (g_a * a_out + g_b * b_out) @ w_out
    h2 = rmsnorm(x, norm_ffn)
    f_gate, f_up = jnp.split(h2 @ w_ffn_in, 2, axis=-1)
    x = x + (jax.nn.silu(f_gate) * f_up) @ w_ffn_out
    return x, st_re.astype(x.dtype), st_im.astype(x.dtype), new_k, new_v


def setup_inputs(seed: int = 0) -> dict:
    key = jax.random.key(seed)
    ks = iter(jax.random.split(key, 32))
    f32 = jnp.float32
    nrm = lambda shape, scale: jax.random.normal(next(ks), shape, f32) * scale
    w_buf = min(WINDOW, PAST_LEN)
    lam_im_base = jnp.pi * jnp.arange(STATE_DIM, dtype=f32)
    return {
        'x_prompt': nrm((BATCH, SEQ, D_MODEL), 1.0),
        'x_sample': nrm((DEC_BATCH, DEC_SEQ, D_MODEL), 1.0),
        'state_ssm_re': nrm((DEPTH, DEC_BATCH, N_GROUPS, STATE_DIM), 0.5),
        'state_ssm_im': nrm((DEPTH, DEC_BATCH, N_GROUPS, STATE_DIM), 0.5),
        'cache_win_k': nrm((DEPTH, DEC_BATCH, w_buf, N_KV_HEADS, HEAD_DIM), 1.0),
        'cache_win_v': nrm((DEPTH, DEC_BATCH, w_buf, N_KV_HEADS, HEAD_DIM), 1.0),
        'rel_bias': nrm((NUM_BUCKETS, N_HEADS), 0.2),
        'norm_attn': 1.0 + nrm((DEPTH, D_MODEL), 0.02),
        'w_in': nrm((DEPTH, D_MODEL, IN_COLS), D_MODEL ** -0.5),
        'lam_re': -0.5 * (1.0 + 0.1 * jax.random.uniform(next(ks), (DEPTH, N_GROUPS, STATE_DIM), f32)),
        'lam_im': lam_im_base + nrm((DEPTH, N_GROUPS, STATE_DIM), 0.01),
        'log_dt': jax.random.uniform(next(ks), (DEPTH, N_GROUPS), f32, math.log(1e-3), math.log(1e-1)),
        'b_re': nrm((DEPTH, N_GROUPS, STATE_DIM, GROUP_CH), (2 * GROUP_CH) ** -0.5),
        'b_im': nrm((DEPTH, N_GROUPS, STATE_DIM, GROUP_CH), (2 * GROUP_CH) ** -0.5),
        'c_re': nrm((DEPTH, N_GROUPS, GROUP_CH, STATE_DIM), STATE_DIM ** -0.5),
        'c_im': nrm((DEPTH, N_GROUPS, GROUP_CH, STATE_DIM), STATE_DIM ** -0.5),
        'd_skip': nrm((DEPTH, SSM_WIDTH), 1.0),
        'w_glu_val': nrm((DEPTH, SSM_WIDTH, D_MODEL), SSM_WIDTH ** -0.5),
        'w_glu_gate': nrm((DEPTH, SSM_WIDTH, D_MODEL), SSM_WIDTH ** -0.5),
        'w_attn_br': nrm((DEPTH, ATTN_WIDTH, D_MODEL), ATTN_WIDTH ** -0.5),
        'sinks': nrm((DEPTH, N_HEADS), 0.5),
        'w_out': nrm((DEPTH, D_MODEL, D_MODEL), D_MODEL ** -0.5),
        'norm_ffn': 1.0 + nrm((DEPTH, D_MODEL), 0.02),
        'w_ffn_in': nrm((DEPTH, D_MODEL, 2 * FFN_HIDDEN), D_MODEL ** -0.5),
        'w_ffn_out': nrm((DEPTH, FFN_HIDDEN, D_MODEL), FFN_HIDDEN ** -0.5),
        'norm_final': 1.0 + nrm((D_MODEL,), 0.02),
    }


def reference(x_prompt, x_sample, state_ssm_re, state_ssm_im, cache_win_k, cache_win_v, rel_bias,
              norm_attn, w_in, lam_re, lam_im, log_dt, b_re, b_im, c_re, c_im, d_skip,
              w_glu_val, w_glu_gate, w_attn_br, sinks, w_out, norm_ffn, w_ffn_in, w_ffn_out,
              norm_final):
    xp, xs = x_prompt, x_sample
    p_re, p_im, p_k, p_v = [], [], [], []
    s_re, s_im, s_k, s_v = [], [], [], []
    for l in range(DEPTH):
        lw = (norm_attn[l], w_in[l], lam_re[l], lam_im[l], log_dt[l], b_re[l], b_im[l],
              c_re[l], c_im[l], d_skip[l], w_glu_val[l], w_glu_gate[l], w_attn_br[l],
              sinks[l], w_out[l], norm_ffn[l], w_ffn_in[l], w_ffn_out[l])
        xp, hr, hi, nk, nv = decoder_layer(xp, None, None, None, None, rel_bias, *lw)
        p_re.append(hr); p_im.append(hi); p_k.append(nk); p_v.append(nv)
        xs, hr, hi, nk, nv = decoder_layer(xs, state_ssm_re[l], state_ssm_im[l],
                                           cache_win_k[l], cache_win_v[l], rel_bias, *lw)
        s_re.append(hr); s_im.append(hi); s_k.append(nk); s_v.append(nv)
    y_prompt = rmsnorm(xp, norm_final)
    y_sample = rmsnorm(xs, norm_final)
    return (y_prompt, y_sample,
            jnp.stack(p_re), jnp.stack(p_im), jnp.stack(p_k), jnp.stack(p_v),
            jnp.stack(s_re), jnp.stack(s_im), jnp.stack(s_k), jnp.stack(s_v))
```

```python
import functools
import math

import numpy as np
import jax
import jax.numpy as jnp
from jax import lax
from jax.experimental import pallas as pl
from jax.experimental.pallas import tpu as pltpu

F32 = jnp.float32
BF16 = jnp.bfloat16

D_MODEL = 2048
SSM_WIDTH = 1024
GROUP_CH = 16
N_GROUPS = 64
STATE_DIM = 64
N_STATE = N_GROUPS * STATE_DIM
HEAD_DIM = 64
N_HEADS = 16
N_KV_HEADS = 4
KV_GROUP = 4
ATTN_WIDTH = 1024
KV_WIDTH = 256
WINDOW = 128
NUM_BUCKETS = 32
MAX_DISTANCE = 128
FFN_HIDDEN = 5632
IN_COLS = 6656
EPS = 1e-6
NEG_INF = -1e30

COL_GA, COL_GB, COL_U, COL_Q, COL_K, COL_V = 0, 2048, 4096, 5120, 6144, 6400

VMEM_LIMIT_BYTES = 56 * 1024 * 1024

SSM_GROUP_BLOCKS = 4
SSM_BLK_CH = SSM_WIDTH // SSM_GROUP_BLOCKS
SSM_BLK_ST = N_STATE // SSM_GROUP_BLOCKS
PROMPT_CHUNK = 32


def _params(sem):
    return pltpu.CompilerParams(dimension_semantics=sem, vmem_limit_bytes=VMEM_LIMIT_BYTES)


def _const_spec(shape):
    nd = len(shape)
    return pl.BlockSpec(shape, lambda *_: (0,) * nd, pipeline_mode=pl.Buffered(1))


def _inproj_kernel(x_ref, g_ref, w_ref, o_ref, h_ref):
    @pl.when(pl.program_id(1) == 0)
    def _():
        x = x_ref[...]
        ms = jnp.mean(x * x, axis=-1, keepdims=True)
        h_ref[...] = (x * lax.rsqrt(ms + EPS) * g_ref[...]).astype(BF16)

    o_ref[...] = jnp.dot(h_ref[...], w_ref[...], preferred_element_type=F32)


def _inproj(x, g, w, tm, tn):
    t = x.shape[0]
    n = w.shape[1]
    return pl.pallas_call(
        _inproj_kernel,
        grid=(t // tm, n // tn),
        in_specs=[
            pl.BlockSpec((tm, D_MODEL), lambda i, j: (i, 0)),
            pl.BlockSpec((1, D_MODEL), lambda i, j: (0, 0)),
            pl.BlockSpec((D_MODEL, tn), lambda i, j: (0, j)),
        ],
        out_specs=pl.BlockSpec((tm, tn), lambda i, j: (i, j)),
        out_shape=jax.ShapeDtypeStruct((t, n), F32),
        scratch_shapes=[pltpu.VMEM((tm, D_MODEL), BF16)],
        compiler_params=_params(("parallel", "arbitrary")),
        name="inproj",
    )(x, g, w)


def _disc_kernel(lre_ref, lim_ref, ldt_ref, are_ref, aim_ref, kre_ref, kim_ref):
    lr = jnp.minimum(lre_ref[...], -1e-4)
    li = lim_ref[...]
    dt = jnp.exp(ldt_ref[...])
    mag = jnp.exp(lr * dt)
    ar = mag * jnp.cos(li * dt)
    ai = mag * jnp.sin(li * dt)
    den = lr * lr + li * li
    are_ref[...] = ar
    aim_ref[...] = ai
    kre_ref[...] = ((ar - 1.0) * lr + ai * li) / den
    kim_ref[...] = (ai * lr - (ar - 1.0) * li) / den


def _discretize(lam_re, lam_im, log_dt):
    shp = jax.ShapeDtypeStruct((N_GROUPS, STATE_DIM), F32)
    ldt = jnp.broadcast_to(log_dt[:, None], (N_GROUPS, STATE_DIM))
    outs = pl.pallas_call(
        _disc_kernel,
        out_shape=(shp, shp, shp, shp),
        name="ssm_discretize",
    )(lam_re, lam_im, ldt)
    return tuple(o.reshape(1, N_STATE) for o in outs)


def _ssm_kernel(*refs, with_y, with_h0):
    it = iter(refs)
    ua_ref, ub_ref, bre_ref, bim_ref = next(it), next(it), next(it), next(it)
    are_ref, aim_ref, kre_ref, kim_ref = next(it), next(it), next(it), next(it)
    if with_y:
        cre_ref, cim_ref, d_ref = next(it), next(it), next(it)
    if with_h0:
        h0re_ref, h0im_ref = next(it), next(it)
    if with_y:
        y_ref = next(it)
    hre_ref, him_ref = next(it), next(it)

    @pl.when(pl.program_id(0) == 0)
    def _():
        if with_h0:
            hre_ref[...] = h0re_ref[...]
            him_ref[...] = h0im_ref[...]
        else:
            hre_ref[...] = jnp.zeros_like(hre_ref)
            him_ref[...] = jnp.zeros_like(him_ref)

    u_halves = (ua_ref[...], ub_ref[...])
    for j in range(SSM_GROUP_BLOCKS):
        ch = slice(SSM_BLK_CH * j, SSM_BLK_CH * (j + 1))
        st = slice(SSM_BLK_ST * j, SSM_BLK_ST * (j + 1))
        half, off = divmod(SSM_BLK_CH * j, SSM_WIDTH // 2)
        u_j = u_halves[half][:, off:off + SSM_BLK_CH]
        uj = u_j.astype(BF16)
        bur = jnp.dot(uj, bre_ref[j], preferred_element_type=F32)
        bui = jnp.dot(uj, bim_ref[j], preferred_element_type=F32)
        kr, ki = kre_ref[:, st], kim_ref[:, st]
        ar, ai = are_ref[:, st], aim_ref[:, st]
        xr = kr * bur - ki * bui
        xi = kr * bui + ki * bur
        hr, hi = hre_ref[:, st], him_ref[:, st]
        nr = ar * hr - ai * hi + xr
        ni = ar * hi + ai * hr + xi
        hre_ref[:, st] = nr
        him_ref[:, st] = ni
        if with_y:
            y = (jnp.dot(nr.astype(BF16), cre_ref[j], preferred_element_type=F32)
                 - jnp.dot(ni.astype(BF16), cim_ref[j], preferred_element_type=F32))
            y_ref[:, ch] = y + d_ref[:, ch] * u_j


def _ssm_pass(proj, steps, bw, disc, cw=None, d_skip=None, h0=None):
    nc = proj.shape[0] // steps
    u_view = proj.reshape(nc, steps * IN_COLS)
    half = SSM_WIDTH // 2
    blocks_per_step, first = IN_COLS // half, COL_U // half
    with_y = cw is not None
    with_h0 = h0 is not None
    ins = [u_view, u_view, bw[0], bw[1], *disc]
    in_specs = [
        pl.BlockSpec((nc, half), lambda s: (0, s * blocks_per_step + first)),
        pl.BlockSpec((nc, half), lambda s: (0, s * blocks_per_step + first + 1)),
        _const_spec(bw[0].shape), _const_spec(bw[1].shape),
    ] + [_const_spec((1, N_STATE))] * 4
    if with_y:
        ins += [cw[0], cw[1], d_skip]
        in_specs += [_const_spec(cw[0].shape), _const_spec(cw[1].shape), _const_spec((1, SSM_WIDTH))]
    if with_h0:
        ins += [h0[0], h0[1]]
        in_specs += [_const_spec((nc, N_STATE))] * 2
    st_shape = jax.ShapeDtypeStruct((nc, N_STATE), F32)
    st_spec = pl.BlockSpec((nc, N_STATE), lambda s: (0, 0))
    out_shape, out_specs = [st_shape, st_shape], [st_spec, st_spec]
    if with_y:
        out_shape = [jax.ShapeDtypeStruct((nc, steps * SSM_WIDTH), F32)] + out_shape
        out_specs = [pl.BlockSpec((nc, SSM_WIDTH), lambda s: (0, s))] + out_specs
    return pl.pallas_call(
        functools.partial(_ssm_kernel, with_y=with_y, with_h0=with_h0),
        grid=(steps,),
        in_specs=in_specs,
        out_specs=out_specs,
        out_shape=out_shape,
        compiler_params=_params(("arbitrary",)),
        name="ssm_scan_y" if with_y else "ssm_scan_state",
    )(*ins)


def _chunk_scan_kernel(sre_ref, sim_ref, are_ref, aim_ref, hre_ref, him_ref, pre_ref, pim_ref, *, n_seq, n_chunks, chunk):
    pr, pi = are_ref[...], aim_ref[...]
    for _ in range(int(math.log2(chunk))):
        pr, pi = pr * pr - pi * pi, 2.0 * pr * pi
    pre_ref[...] = pr
    pim_ref[...] = pi
    zero = jnp.zeros((1, N_STATE), F32)
    for b in range(n_seq):
        hre_ref[pl.ds(b * n_chunks, 1), :] = zero
        him_ref[pl.ds(b * n_chunks, 1), :] = zero

    def body(c, carry):
        for b in range(n_seq):
            r = b * n_chunks + c
            hpr, hpi = hre_ref[pl.ds(r - 1, 1), :], him_ref[pl.ds(r - 1, 1), :]
            spr, spi = sre_ref[pl.ds(r - 1, 1), :], sim_ref[pl.ds(r - 1, 1), :]
            p_r, p_i = pre_ref[...], pim_ref[...]
            hre_ref[pl.ds(r, 1), :] = p_r * hpr - p_i * hpi + spr
            him_ref[pl.ds(r, 1), :] = p_r * hpi + p_i * hpr + spi
        return carry

    lax.fori_loop(1, n_chunks, body, 0)


def _chunk_scan(s_re, s_im, a_re, a_im, n_seq, chunk):
    nc = s_re.shape[0]
    shp = jax.ShapeDtypeStruct((nc, N_STATE), F32)
    return pl.pallas_call(
        functools.partial(_chunk_scan_kernel, n_seq=n_seq, n_chunks=nc // n_seq, chunk=chunk),
        out_shape=(shp, shp),
        scratch_shapes=[pltpu.VMEM((1, N_STATE), F32), pltpu.VMEM((1, N_STATE), F32)],
        compiler_params=pltpu.CompilerParams(vmem_limit_bytes=VMEM_LIMIT_BYTES),
        name="ssm_chunk_scan",
    )(s_re, s_im, a_re, a_im)


def _softmax_sink(s, sink):
    m = jnp.maximum(jnp.max(s, axis=-1, keepdims=True), sink)
    p = jnp.exp(s - m)
    den = jnp.sum(p, axis=-1, keepdims=True) + jnp.exp(sink - m)
    return p / den


def _attn_prompt_kernel(sink_ref, q_ref, kp_ref, kc_ref, vp_ref, vc_ref, bias_ref, o_ref):
    blk = pl.program_id(1)
    q = q_ref[...]
    k = jnp.concatenate([kp_ref[...], kc_ref[...]], axis=0).astype(BF16)
    v = jnp.concatenate([vp_ref[...], vc_ref[...]], axis=0).astype(BF16)
    qi = lax.broadcasted_iota(jnp.int32, (WINDOW, 2 * WINDOW), 0)
    kj = lax.broadcasted_iota(jnp.int32, (WINDOW, 2 * WINDOW), 1)
    dist = qi + WINDOW - kj
    kpos = (blk - 1) * WINDOW + kj
    valid = (dist >= 0) & (dist < WINDOW) & (kpos >= 0)
    for h in range(N_KV_HEADS):
        kh = k[:, HEAD_DIM * h:HEAD_DIM * (h + 1)]
        vh = v[:, HEAD_DIM * h:HEAD_DIM * (h + 1)]
        for g in range(KV_GROUP):
            head = KV_GROUP * h + g
            qh = q[:, HEAD_DIM * head:HEAD_DIM * (head + 1)].astype(BF16)
            s = lax.dot_general(qh, kh, (((1,), (1,)), ((), ())), preferred_element_type=F32)
            s = s * (HEAD_DIM ** -0.5) + bias_ref[head]
            s = jnp.where(valid, s, NEG_INF)
            p = _softmax_sink(s, sink_ref[head])
            o_ref[:, HEAD_DIM * head:HEAD_DIM * (head + 1)] = jnp.dot(
                p.astype(BF16), vh, preferred_element_type=F32)


def _attn_prompt(proj, sinks, bias, n_seq, seq_len):
    nb = seq_len // WINDOW
    qcol, kcol, vcol = COL_Q // ATTN_WIDTH, COL_K // KV_WIDTH, COL_V // KV_WIDTH
    cur = lambda b, i: b * nb + i
    prev = lambda b, i: b * nb + jnp.maximum(i - 1, 0)
    return pl.pallas_call(
        _attn_prompt_kernel,
        grid=(n_seq, nb),
        in_specs=[
            pl.BlockSpec(memory_space=pltpu.SMEM),
            pl.BlockSpec((WINDOW, ATTN_WIDTH), lambda b, i: (cur(b, i), qcol)),
            pl.BlockSpec((WINDOW, KV_WIDTH), lambda b, i: (prev(b, i), kcol)),
            pl.BlockSpec((WINDOW, KV_WIDTH), lambda b, i: (cur(b, i), kcol)),
            pl.BlockSpec((WINDOW, KV_WIDTH), lambda b, i: (prev(b, i), vcol)),
            pl.BlockSpec((WINDOW, KV_WIDTH), lambda b, i: (cur(b, i), vcol)),
            _const_spec(bias.shape),
        ],
        out_specs=pl.BlockSpec((WINDOW, ATTN_WIDTH), lambda b, i: (cur(b, i), 0)),
        out_shape=jax.ShapeDtypeStruct((n_seq * seq_len, ATTN_WIDTH), F32),
        compiler_params=_params(("parallel", "arbitrary")),
        name="attn_prompt",
    )(sinks, proj, proj, proj, proj, proj, bias)


def _attn_sample_kernel(sink_ref, q_ref, kb_ref, vb_ref, kn_ref, vn_ref, bb_ref, bn_ref, o_ref, *, nb, nq):
    rows = KV_GROUP * nq
    ri = lax.rem(lax.broadcasted_iota(jnp.int32, (rows, WINDOW), 0), nq)
    kj = lax.broadcasted_iota(jnp.int32, (rows, WINDOW), 1)
    valid_buf = kj >= ri + 1
    r1 = lax.broadcasted_iota(jnp.int32, (rows, 1), 0)
    rq = lax.rem(r1, nq)
    rg = r1 // nq
    dn = (((1,), (1,)), ((), ()))
    scale = HEAD_DIM ** -0.5
    for n in range(nb):
        for h in range(N_KV_HEADS):
            hs = slice(HEAD_DIM * h, HEAD_DIM * (h + 1))
            q = q_ref[n, h]
            kb = kb_ref[n, :, hs].astype(BF16)
            vb = vb_ref[n, :, hs].astype(BF16)
            kn = kn_ref[n, :, hs]
            vn = vn_ref[n, :, hs]
            sink = jnp.zeros((rows, 1), F32)
            for g in range(KV_GROUP):
                sink = jnp.where(rg == g, sink_ref[KV_GROUP * h + g], sink)
            sb = lax.dot_general(q.astype(BF16), kb, dn, preferred_element_type=F32) * scale + bb_ref[h]
            sb = jnp.where(valid_buf, sb, NEG_INF)
            m = jnp.maximum(jnp.max(sb, axis=-1, keepdims=True), sink)
            sn = []
            for j in range(nq):
                s_j = jnp.sum(q * kn[j:j + 1, :], axis=-1, keepdims=True) * scale + bn_ref[h, :, j:j + 1]
                s_j = jnp.where(rq >= j, s_j, NEG_INF)
                sn.append(s_j)
                m = jnp.maximum(m, s_j)
            pb = jnp.exp(sb - m)
            den = jnp.sum(pb, axis=-1, keepdims=True) + jnp.exp(sink - m)
            pn = [jnp.exp(s_j - m) for s_j in sn]
            for p_j in pn:
                den = den + p_j
            o = jnp.dot((pb / den).astype(BF16), vb, preferred_element_type=F32)
            for j in range(nq):
                o = o + (pn[j] / den) * vn[j:j + 1, :]
            o_ref[n, h] = o


def _attn_sample(q16, kbuf, vbuf, knew, vnew, sinks, bias_buf, bias_new, nb):
    n, nq = kbuf.shape[0], knew.shape[1]
    rows = KV_GROUP * nq
    return pl.pallas_call(
        functools.partial(_attn_sample_kernel, nb=nb, nq=nq),
        grid=(n // nb,),
        in_specs=[
            pl.BlockSpec(memory_space=pltpu.SMEM),
            pl.BlockSpec((nb, N_KV_HEADS, rows, HEAD_DIM), lambda i: (i, 0, 0, 0)),
            pl.BlockSpec((nb, WINDOW, KV_WIDTH), lambda i: (i, 0, 0)),
            pl.BlockSpec((nb, WINDOW, KV_WIDTH), lambda i: (i, 0, 0)),
            pl.BlockSpec((nb, nq, KV_WIDTH), lambda i: (i, 0, 0)),
            pl.BlockSpec((nb, nq, KV_WIDTH), lambda i: (i, 0, 0)),
            _const_spec(bias_buf.shape),
            _const_spec(bias_new.shape),
        ],
        out_specs=pl.BlockSpec((nb, N_KV_HEADS, rows, HEAD_DIM), lambda i: (i, 0, 0, 0)),
        out_shape=jax.ShapeDtypeStruct((n, N_KV_HEADS, rows, HEAD_DIM), F32),
        compiler_params=_params(("parallel",)),
        name="attn_sample",
    )(sinks, q16, kbuf, vbuf, knew, vnew, bias_buf, bias_new)


def _merge_kernel(y_ref, o_ref, ga_ref, gb_ref, wv_ref, wg_ref, wb_ref, m_ref, g_ref, ob_ref):
    @pl.when(pl.program_id(1) == 0)
    def _():
        g_ref[...] = jax.nn.gelu(y_ref[...]).astype(BF16)
        ob_ref[...] = o_ref[...].astype(BF16)

    g = g_ref[...]
    a_out = (jnp.dot(g, wv_ref[...], preferred_element_type=F32)
             * jax.nn.sigmoid(jnp.dot(g, wg_ref[...], preferred_element_type=F32)))
    b_out = jnp.dot(ob_ref[...], wb_ref[...], preferred_element_type=F32)
    m = jax.nn.sigmoid(ga_ref[...]) * a_out + jax.nn.sigmoid(gb_ref[...]) * b_out
    m_ref[...] = m.astype(BF16)


def _merge(y, o, proj, wv, wg, wb, tm, tn):
    t = y.shape[0]
    nj = D_MODEL // tn
    w_spec = pl.BlockSpec((SSM_WIDTH, tn), lambda i, j: (0, j))
    return pl.pallas_call(
        _merge_kernel,
        grid=(t // tm, nj),
        in_specs=[
            pl.BlockSpec((tm, SSM_WIDTH), lambda i, j: (i, 0)),
            pl.BlockSpec((tm, ATTN_WIDTH), lambda i, j: (i, 0)),
            pl.BlockSpec((tm, tn), lambda i, j: (i, COL_GA // tn + j)),
            pl.BlockSpec((tm, tn), lambda i, j: (i, COL_GB // tn + j)),
            w_spec, w_spec, w_spec,
        ],
        out_specs=pl.BlockSpec((tm, tn), lambda i, j: (i, j)),
        out_shape=jax.ShapeDtypeStruct((t, D_MODEL), BF16),
        scratch_shapes=[pltpu.VMEM((tm, SSM_WIDTH), BF16), pltpu.VMEM((tm, ATTN_WIDTH), BF16)],
        compiler_params=_params(("parallel", "arbitrary")),
        name="merge",
    )(y, o, proj, proj, wv, wg, wb)


def _outproj_kernel(m_ref, w_ref, x_ref, o_ref):
    o_ref[...] = x_ref[...] + jnp.dot(m_ref[...], w_ref[...], preferred_element_type=F32)


def _outproj(m, w, x, tm, tn):
    t = m.shape[0]
    return pl.pallas_call(
        _outproj_kernel,
        grid=(t // tm, D_MODEL // tn),
        in_specs=[
            pl.BlockSpec((tm, D_MODEL), lambda i, j: (i, 0)),
            pl.BlockSpec((D_MODEL, tn), lambda i, j: (0, j)),
            pl.BlockSpec((tm, tn), lambda i, j: (i, j)),
        ],
        out_specs=pl.BlockSpec((tm, tn), lambda i, j: (i, j)),
        out_shape=jax.ShapeDtypeStruct((t, D_MODEL), F32),
        compiler_params=_params(("parallel", "arbitrary")),
        name="outproj",
    )(m, w, x)


def _ffn_kernel(x_ref, gn_ref, wg_ref, wu_ref, wo_ref, gf_ref, o_ref, h_ref, acc_ref):
    f = pl.program_id(1)

    @pl.when(f == 0)
    def _():
        x = x_ref[...]
        ms = jnp.mean(x * x, axis=-1, keepdims=True)
        h_ref[...] = (x * lax.rsqrt(ms + EPS) * gn_ref[...]).astype(BF16)
        acc_ref[...] = jnp.zeros_like(acc_ref)

    h = h_ref[...]
    gate = jnp.dot(h, wg_ref[...], preferred_element_type=F32)
    up = jnp.dot(h, wu_ref[...], preferred_element_type=F32)
    act = (jax.nn.silu(gate) * up).astype(BF16)
    acc_ref[...] += jnp.dot(act, wo_ref[...], preferred_element_type=F32)

    @pl.when(f == pl.num_programs(1) - 1)
    def _():
        x = x_ref[...] + acc_ref[...]
        ms = jnp.mean(x * x, axis=-1, keepdims=True)
        o_ref[...] = x * lax.rsqrt(ms + EPS) * gf_ref[...]


def _ffn(x, gn, w_in, w_out, gf, tm, tf):
    t = x.shape[0]
    nf = FFN_HIDDEN // tf
    return pl.pallas_call(
        _ffn_kernel,
        grid=(t // tm, nf),
        in_specs=[
            pl.BlockSpec((tm, D_MODEL), lambda i, f: (i, 0)),
            pl.BlockSpec((1, D_MODEL), lambda i, f: (0, 0)),
            pl.BlockSpec((D_MODEL, tf), lambda i, f: (0, f)),
            pl.BlockSpec((D_MODEL, tf), lambda i, f: (0, nf + f)),
            pl.BlockSpec((tf, D_MODEL), lambda i, f: (f, 0)),
            pl.BlockSpec((1, D_MODEL), lambda i, f: (0, 0)),
        ],
        out_specs=pl.BlockSpec((tm, D_MODEL), lambda i, f: (i, 0)),
        out_shape=jax.ShapeDtypeStruct((t, D_MODEL), F32),
        scratch_shapes=[pltpu.VMEM((tm, D_MODEL), BF16), pltpu.VMEM((tm, D_MODEL), F32)],
        compiler_params=_params(("parallel", "arbitrary")),
        name="ffn",
    )(x, gn, w_in, w_in, w_out, gf)


def _t5_bucket(dist):
    n = np.maximum(dist, 0)
    max_exact = NUM_BUCKETS // 2
    large = max_exact + (np.log(np.maximum(n, 1) / max_exact) / np.log(MAX_DISTANCE / max_exact)
                         * (NUM_BUCKETS - max_exact)).astype(np.int32)
    large = np.minimum(large, NUM_BUCKETS - 1)
    return np.where(n < max_exact, n, large).astype(np.int32)


def _block_diag(w):
    g, a, b = w.shape
    nblk = g // GROUP_CH
    wb = w.reshape(nblk, GROUP_CH, 1, a, b)
    eye = jnp.eye(GROUP_CH, dtype=bool).reshape(1, GROUP_CH, GROUP_CH, 1, 1)
    full = jnp.where(eye, wb, 0.0)
    full = full.transpose(0, 1, 3, 2, 4)
    return full.reshape(nblk, GROUP_CH * a, GROUP_CH * b).astype(BF16)


def _layer(x2d, proj, y, o, wv, wg, wb, wo, norm_ffn, w_ffn_in, w_ffn_out, norm_final, tm):
    m = _merge(y, o, proj, wv, wg, wb, tm, 512)
    x2 = _outproj(m, wo, x2d, tm, 1024)
    return _ffn(x2, norm_ffn, w_ffn_in, w_ffn_out, norm_final, tm, 512)


def kernel(x_prompt, x_sample, state_ssm_re, state_ssm_im, cache_win_k, cache_win_v, rel_bias, norm_attn, w_in, lam_re, lam_im, log_dt, b_re, b_im, c_re, c_im, d_skip, w_glu_val, w_glu_gate, w_attn_br, sinks, w_out, norm_ffn, w_ffn_in, w_ffn_out, norm_final):
    n_p, len_p, _ = x_prompt.shape
    n_s, len_s, _ = x_sample.shape
    w_buf = cache_win_k.shape[2]

    cs = np.cumsum([0, SSM_WIDTH, ATTN_WIDTH, KV_WIDTH, KV_WIDTH, D_MODEL, D_MODEL])
    w0 = w_in[0]
    w_in_p = jnp.concatenate(
        [w0[:, cs[4]:cs[5]], w0[:, cs[5]:cs[6]], w0[:, cs[0]:cs[1]], w0[:, cs[1]:cs[2]],
         w0[:, cs[2]:cs[3]], w0[:, cs[3]:cs[4]]], axis=1).astype(BF16)
    wv, wg, wb = w_glu_val[0].astype(BF16), w_glu_gate[0].astype(BF16), w_attn_br[0].astype(BF16)
    wo = w_out[0].astype(BF16)
    wfi, wfo = w_ffn_in[0].astype(BF16), w_ffn_out[0].astype(BF16)
    g_attn, g_ffn, g_fin = norm_attn[0][None, :], norm_ffn[0][None, :], norm_final[None, :]
    bw = (_block_diag(b_re[0].transpose(0, 2, 1)), _block_diag(b_im[0].transpose(0, 2, 1)))
    cw = (_block_diag(c_re[0].transpose(0, 2, 1)), _block_diag(c_im[0].transpose(0, 2, 1)))
    dsk = d_skip[0][None, :]
    disc = _discretize(lam_re[0], lam_im[0], log_dt[0])
    a_re, a_im = disc[0], disc[1]
    sk = sinks[0]

    xp = x_prompt.reshape(n_p * len_p, D_MODEL)
    proj_p = _inproj(xp, g_attn, w_in_p, 512, 1664)
    n_chunks = len_p // PROMPT_CHUNK
    s_re, s_im = _ssm_pass(proj_p, PROMPT_CHUNK, bw, disc)
    h0 = _chunk_scan(s_re, s_im, a_re, a_im, n_p, PROMPT_CHUNK)
    y_p, hf_re, hf_im = _ssm_pass(proj_p, PROMPT_CHUNK, bw, disc, cw, dsk, h0)
    y_p = y_p.reshape(n_p * len_p, SSM_WIDTH)
    last = np.arange(n_p) * n_chunks + n_chunks - 1
    p_re = hf_re[last].reshape(1, n_p, N_GROUPS, STATE_DIM)
    p_im = hf_im[last].reshape(1, n_p, N_GROUPS, STATE_DIM)

    ip = np.arange(WINDOW)[:, None]
    jp = np.arange(2 * WINDOW)[None, :]
    bias_p = jnp.transpose(rel_bias[_t5_bucket(ip + WINDOW - jp)], (2, 0, 1))
    o_p = _attn_prompt(proj_p, sk, bias_p, n_p, len_p)
    yp = _layer(xp, proj_p, y_p, o_p, wv, wg, wb, wo, g_ffn, wfi, wfo, g_fin, 512)
    kv_p = proj_p.reshape(n_p, len_p, IN_COLS)[:, len_p - w_buf:, COL_K:COL_K + 2 * KV_WIDTH]
    p_k = kv_p[..., :KV_WIDTH].reshape(1, n_p, w_buf, N_KV_HEADS, HEAD_DIM)
    p_v = kv_p[..., KV_WIDTH:].reshape(1, n_p, w_buf, N_KV_HEADS, HEAD_DIM)

    xs = x_sample.reshape(n_s * len_s, D_MODEL)
    proj_s = _inproj(xs, g_attn, w_in_p, 512, 1664)
    us_view = proj_s.reshape(n_s, len_s, IN_COLS)
    h0s = (state_ssm_re[0].reshape(n_s, N_STATE), state_ssm_im[0].reshape(n_s, N_STATE))
    y_s, hs_re, hs_im = _ssm_pass(proj_s, len_s, bw, disc, cw, dsk, h0s)
    y_s = y_s.reshape(n_s * len_s, SSM_WIDTH)
    s_re_out = hs_re.reshape(1, n_s, N_GROUPS, STATE_DIM)
    s_im_out = hs_im.reshape(1, n_s, N_GROUPS, STATE_DIM)

    q_s = us_view[:, :, COL_Q:COL_Q + ATTN_WIDTH].reshape(n_s, len_s, N_KV_HEADS, KV_GROUP, HEAD_DIM)
    q16 = q_s.transpose(0, 2, 3, 1, 4).reshape(n_s, N_KV_HEADS, KV_GROUP * len_s, HEAD_DIM)
    k_new = us_view[:, :, COL_K:COL_K + KV_WIDTH]
    v_new = us_view[:, :, COL_V:COL_V + KV_WIDTH]
    kbuf = cache_win_k[0].reshape(n_s, w_buf, KV_WIDTH)
    vbuf = cache_win_v[0].reshape(n_s, w_buf, KV_WIDTH)
    i_s = np.arange(len_s)[:, None]
    j_s = np.arange(w_buf + len_s)[None, :]
    bias_s = rel_bias[_t5_bucket(i_s + w_buf - j_s)]
    bias_s = bias_s.reshape(len_s, w_buf + len_s, N_KV_HEADS, KV_GROUP)
    bias_s = bias_s.transpose(2, 3, 0, 1).reshape(N_KV_HEADS, KV_GROUP * len_s, w_buf + len_s)
    o16 = _attn_sample(q16, kbuf, vbuf, k_new, v_new, sk,
                       bias_s[..., :w_buf], bias_s[..., w_buf:], 8)
    o_s = o16.reshape(n_s, N_KV_HEADS, KV_GROUP, len_s, HEAD_DIM).transpose(0, 3, 1, 2, 4)
    o_s = o_s.reshape(n_s * len_s, ATTN_WIDTH)
    ys = _layer(xs, proj_s, y_s, o_s, wv, wg, wb, wo, g_ffn, wfi, wfo, g_fin, 512)
    s_k = jnp.concatenate([kbuf, k_new], axis=1)[:, -w_buf:].reshape(1, n_s, w_buf, N_KV_HEADS, HEAD_DIM)
    s_v = jnp.concatenate([vbuf, v_new], axis=1)[:, -w_buf:].reshape(1, n_s, w_buf, N_KV_HEADS, HEAD_DIM)

    return (yp.reshape(n_p, len_p, D_MODEL), ys.reshape(n_s, len_s, D_MODEL),
            p_re, p_im, p_k, p_v, s_re_out, s_im_out, s_k, s_v)
```
